```python
import math
import jax, jax.numpy as jnp
from jax import lax
import numpy as np

D_MODEL = 2048
BATCH = 4
SEQ = 2048
DEPTH = 1

N_HEADS_A = 8
HEAD_DIM_A = 128
WIDTH_A = N_HEADS_A * HEAD_DIM_A
N_IDX_HEADS = 16
IDX_DIM = 64
TOPK_MAX = 256
Q_BLOCK = 128
N_GROUPS_B = 8
GROUP_DIM_B = 128
WIDTH_B = N_GROUPS_B * GROUP_DIM_B
CHUNK = 128
D_FF = ((8 * D_MODEL + 3 * 256 - 1) // (3 * 256)) * 256
N_BUCKETS = 32
MAX_DISTANCE = 128
EPS = 1e-6
IN_SPLITS = (WIDTH_A, WIDTH_A, WIDTH_A, N_IDX_HEADS * IDX_DIM, IDX_DIM, N_IDX_HEADS,
             WIDTH_B, WIDTH_B, D_MODEL, D_MODEL)
D_IN = sum(IN_SPLITS)

kernel_name = "hybrid_dsa_gmlp_gated_block"


def _rmsnorm(x, g):
    xf = x.astype(jnp.float32)
    y = xf * lax.rsqrt(jnp.mean(xf * xf, axis=-1, keepdims=True) + EPS)
    return (y * g.astype(jnp.float32)).astype(x.dtype)


def _t5_bucket(dist):
    max_exact = N_BUCKETS // 2
    d = jnp.maximum(dist.astype(jnp.float32), 1.0)
    large = max_exact + (jnp.log(d / max_exact) / math.log(MAX_DISTANCE / max_exact)
                         * (N_BUCKETS - max_exact)).astype(jnp.int32)
    large = jnp.minimum(large, N_BUCKETS - 1)
    return jnp.where(dist < max_exact, dist, large)


def _split_cols(proj):
    offs = np.cumsum(IN_SPLITS)[:-1].tolist()
    return jnp.split(proj, offs, axis=-1)


def _sparse_attention(q, k, v, q_idx, k_idx, w_idx, rel_bias):
    B, S = q.shape[0], q.shape[1]
    n_sel = min(TOPK_MAX, S // 4)
    n_blocks = S // Q_BLOCK
    scale = HEAD_DIM_A ** -0.5
    idx_scale = (IDX_DIM ** -0.5) * (N_IDX_HEADS ** -0.5)
    key_pos = jnp.arange(S, dtype=jnp.int32)
    b_ix = jnp.arange(B)[:, None, None]
    k_idx_f = k_idx.astype(jnp.float32)

    def block(i):
        t0 = i * Q_BLOCK
        qb = lax.dynamic_slice_in_dim(q, t0, Q_BLOCK, axis=1)
        qib = lax.dynamic_slice_in_dim(q_idx, t0, Q_BLOCK, axis=1).astype(jnp.float32)
        wib = lax.dynamic_slice_in_dim(w_idx, t0, Q_BLOCK, axis=1).astype(jnp.float32)
        q_pos = t0 + jnp.arange(Q_BLOCK, dtype=jnp.int32)
        head_scores = jax.nn.relu(jnp.einsum('bqhd,bsd->bqhs', qib, k_idx_f))
        score = jnp.einsum('bqhs,bqh->bqs', head_scores, wib) * idx_scale
        causal = key_pos[None, :] <= q_pos[:, None]
        score = jnp.where(causal[None], score, -jnp.inf)
        _, sel = lax.top_k(score, n_sel)
        k_sel = k[b_ix, sel]
        v_sel = v[b_ix, sel]
        dist = q_pos[None, :, None] - sel
        valid = dist >= 0
        bias = rel_bias[_t5_bucket(jnp.maximum(dist, 0))]
        logits = (jnp.einsum('bqhd,bqkhd->bhqk', qb, k_sel).astype(jnp.float32) * scale
                  + bias.astype(jnp.float32).transpose(0, 3, 1, 2))
        logits = jnp.where(valid[:, None], logits, -jnp.inf)
        p = jax.nn.softmax(logits, axis=-1).astype(v.dtype)
        return jnp.einsum('bhqk,bqkhd->bqhd', p, v_sel)

    out = lax.map(block, jnp.arange(n_blocks, dtype=jnp.int32))
    return out.transpose(1, 0, 2, 3, 4).reshape(B, S, WIDTH_A)


def _chunked_sgu(u, v, w_spatial, b_spatial, norm_g):
    B, S = u.shape[0], u.shape[1]
    v = _rmsnorm(v, norm_g)
    vc = v.reshape(B, S // CHUNK, CHUNK, N_GROUPS_B, GROUP_DIM_B)
    mask = jnp.tril(jnp.ones((CHUNK, CHUNK), dtype=bool))
    w = jnp.where(mask[None], w_spatial, 0.0)
    z = jnp.einsum('gts,bcsgd->bctgd', w, vc) + b_spatial.T[:, :, None]
    return u * z.reshape(B, S, WIDTH_B)


def setup_inputs(seed: int = 0) -> dict:
    key = jax.random.key(seed)
    ks = jax.random.split(key, 17)

    def nrm(k, shape, scale):
        return jax.random.normal(k, shape, jnp.float32) * scale

    return {
        "x": nrm(ks[0], (BATCH, SEQ, D_MODEL), 1.0),
        "norm1_g": 1.0 + nrm(ks[1], (DEPTH, D_MODEL), 0.02),
        "w_in": nrm(ks[2], (DEPTH, D_MODEL, D_IN), D_MODEL ** -0.5),
        "q_norm_g": 1.0 + nrm(ks[3], (DEPTH, HEAD_DIM_A), 0.02),
        "k_norm_g": 1.0 + nrm(ks[4], (DEPTH, HEAD_DIM_A), 0.02),
        "idx_k_norm_g": 1.0 + nrm(ks[5], (DEPTH, IDX_DIM), 0.02),
        "sgu_norm_g": 1.0 + nrm(ks[6], (DEPTH, WIDTH_B), 0.02),
        "w_spatial": nrm(ks[7], (DEPTH, N_GROUPS_B, CHUNK, CHUNK), 0.5 * CHUNK ** -0.5),
        "b_spatial": 1.0 + nrm(ks[8], (DEPTH, N_GROUPS_B, CHUNK), 0.1),
        "w_proj_a": nrm(ks[9], (DEPTH, WIDTH_A, D_MODEL), WIDTH_A ** -0.5),
        "w_proj_b": nrm(ks[10], (DEPTH, WIDTH_B, D_MODEL), WIDTH_B ** -0.5),
        "w_out": nrm(ks[11], (DEPTH, D_MODEL, D_MODEL), D_MODEL ** -0.5),
        "norm2_g": 1.0 + nrm(ks[12], (DEPTH, D_MODEL), 0.02),
        "w_ffn_gate": nrm(ks[13], (DEPTH, D_MODEL, D_FF), D_MODEL ** -0.5),
        "w_ffn_up": nrm(ks[14], (DEPTH, D_MODEL, D_FF), D_MODEL ** -0.5),
        "w_ffn_down": nrm(ks[15], (DEPTH, D_FF, D_MODEL), D_FF ** -0.5),
        "rel_bias": nrm(ks[16], (N_BUCKETS, N_HEADS_A), 0.5),
    }


def reference(x, norm1_g, w_in, q_norm_g, k_norm_g, idx_k_norm_g, sgu_norm_g,
              w_spatial, b_spatial, w_proj_a, w_proj_b, w_out, norm2_g,
              w_ffn_gate, w_ffn_up, w_ffn_down, rel_bias):
    B, S = x.shape[0], x.shape[1]
    for l in range(DEPTH):
        h = _rmsnorm(x, norm1_g[l])
        proj = h @ w_in[l]
        q, k, v, q_i, k_i, w_i, u_b, v_b, g_a, g_b = _split_cols(proj)
        q = _rmsnorm(q.reshape(B, S, N_HEADS_A, HEAD_DIM_A), q_norm_g[l])
        k = _rmsnorm(k.reshape(B, S, N_HEADS_A, HEAD_DIM_A), k_norm_g[l])
        v = v.reshape(B, S, N_HEADS_A, HEAD_DIM_A)
        q_i = q_i.reshape(B, S, N_IDX_HEADS, IDX_DIM)
        k_i = _rmsnorm(k_i, idx_k_norm_g[l])
        out_a = _sparse_attention(q, k, v, q_i, k_i, w_i, rel_bias)
        out_b = _chunked_sgu(jax.nn.gelu(u_b), jax.nn.gelu(v_b),
                             w_spatial[l], b_spatial[l], sgu_norm_g[l])
        merged = (jax.nn.sigmoid(g_a) * (out_a @ w_proj_a[l])
                  + jax.nn.sigmoid(g_b) * (out_b @ w_proj_b[l]))
        x = x + merged @ w_out[l]
        h = _rmsnorm(x, norm2_g[l])
        x = x + (jax.nn.silu(h @ w_ffn_gate[l]) * (h @ w_ffn_up[l])) @ w_ffn_down[l]
    return x
```

```python
import functools
import math

import jax
import jax.numpy as jnp
from jax import lax
from jax.experimental import pallas as pl
from jax.experimental.pallas import tpu as pltpu

N_HEADS = 8
HEAD_DIM = 128
N_IDX_HEADS = 16
IDX_DIM = 64
TOPK_MAX = 256
CHUNK = 128
N_GROUPS = 8
N_BUCKETS = 32
MAX_DISTANCE = 128
EPS = 1e-6

V7X_LANES = 128
V7X_SUBLANES = 8
V7X_VMEM_BYTES = 64 * 1024 * 1024
V7X_VMEM_USABLE = V7X_VMEM_BYTES - 8 * 1024 * 1024

TM = 512
TQ = 256
TK = 256
TF = 512
TN_MERGE = 1024
BISECT_ITERS = 32
MASK_FILL = -1e30

F32 = jnp.float32
BF16 = jnp.bfloat16
_NT = (((1,), (1,)), ((), ()))


def _nbytes(shape, dtype):
    return math.prod(shape) * jnp.dtype(dtype).itemsize


def _params(n_grid, block_bytes, temp_bytes):
    need = 2 * block_bytes + temp_bytes + (4 << 20)
    limit = min(max(need, 32 << 20), V7X_VMEM_USABLE)
    return pltpu.CompilerParams(dimension_semantics=("arbitrary",) * n_grid, vmem_limit_bytes=limit)


def _rmsnorm_kernel(x_ref, g_ref, o_ref):
    x = x_ref[...]
    ms = jnp.mean(x * x, axis=-1, keepdims=True)
    o_ref[...] = (x * lax.rsqrt(ms + EPS) * g_ref[...]).astype(o_ref.dtype)


def _rmsnorm(x, g):
    t, d = x.shape
    blk = _nbytes((TM, d), F32) + _nbytes((TM, d), BF16)
    return pl.pallas_call(
        _rmsnorm_kernel,
        out_shape=jax.ShapeDtypeStruct((t, d), BF16),
        grid=(t // TM,),
        in_specs=[pl.BlockSpec((TM, d), lambda i: (i, 0)), pl.BlockSpec((1, d), lambda i: (0, 0))],
        out_specs=pl.BlockSpec((TM, d), lambda i: (i, 0)),
        compiler_params=_params(1, blk, 2 * _nbytes((TM, d), F32)),
        name="rmsnorm1",
    )(x, g)


def _head_rmsnorm(y, g, post_scale):
    outs = []
    for hd in range(y.shape[1] // HEAD_DIM):
        yh = y[:, hd * HEAD_DIM:(hd + 1) * HEAD_DIM]
        ms = jnp.mean(yh * yh, axis=-1, keepdims=True)
        outs.append(yh * (lax.rsqrt(ms + EPS) * post_scale) * g)
    return jnp.concatenate(outs, axis=1)


def _proj_kernel(h_ref, wq_ref, wk_ref, wvt_ref, wqi_ref, wklo_ref, wkhi_ref, wwt_ref,
                 gq_ref, gk_ref, gklo_ref, gkhi_ref,
                 q_ref, k_ref, vt_ref, qi_ref, klo_ref, khi_ref, wt_ref):
    h = h_ref[...]
    q = jnp.dot(h, wq_ref[...], preferred_element_type=F32)
    q_ref[...] = _head_rmsnorm(q, gq_ref[...], HEAD_DIM ** -0.5).astype(q_ref.dtype)
    k = jnp.dot(h, wk_ref[...], preferred_element_type=F32)
    k_ref[...] = _head_rmsnorm(k, gk_ref[...], 1.0).astype(k_ref.dtype)
    vt = lax.dot_general(wvt_ref[...], h, _NT, preferred_element_type=F32).astype(vt_ref.dtype)
    for c in range(vt_ref.shape[0]):
        vt_ref[c] = vt[:, c * TK:(c + 1) * TK]
    qi_ref[...] = jnp.dot(h, wqi_ref[...], preferred_element_type=F32).astype(qi_ref.dtype)
    for w_ref, g_ref, o_ref in ((wklo_ref, gklo_ref, klo_ref), (wkhi_ref, gkhi_ref, khi_ref)):
        y = jnp.dot(h, w_ref[...], preferred_element_type=F32)
        ms = jnp.sum(y * y, axis=-1, keepdims=True) * (1.0 / IDX_DIM)
        o_ref[...] = (y * lax.rsqrt(ms + EPS) * g_ref[...]).astype(o_ref.dtype)
    idx_scale = (IDX_DIM ** -0.5) * (N_IDX_HEADS ** -0.5)
    wt_ref[...] = lax.dot_general(wwt_ref[...], h, _NT, preferred_element_type=F32) * idx_scale


def _projections(h, wq, wk, wvt, wqi, wklo, wkhi, wwt, gq, gk, gklo, gkhi):
    t, d = h.shape
    wa = wq.shape[1]
    wi = wqi.shape[1]
    const = lambda i: (0, 0)
    row = lambda i: (i, 0)
    in_specs = [
        pl.BlockSpec((TM, d), row),
        pl.BlockSpec((d, wa), const), pl.BlockSpec((d, wa), const), pl.BlockSpec((wa, d), const),
        pl.BlockSpec((d, wi), const), pl.BlockSpec((d, V7X_LANES), const), pl.BlockSpec((d, V7X_LANES), const),
        pl.BlockSpec((N_IDX_HEADS, d), const),
        pl.BlockSpec((1, HEAD_DIM), const), pl.BlockSpec((1, HEAD_DIM), const),
        pl.BlockSpec((1, V7X_LANES), const), pl.BlockSpec((1, V7X_LANES), const),
    ]
    out_shape = [
        jax.ShapeDtypeStruct((t, wa), BF16),
        jax.ShapeDtypeStruct((t, wa), BF16),
        jax.ShapeDtypeStruct((t // TK, wa, TK), BF16),
        jax.ShapeDtypeStruct((t, wi), BF16),
        jax.ShapeDtypeStruct((t, V7X_LANES), BF16),
        jax.ShapeDtypeStruct((t, V7X_LANES), BF16),
        jax.ShapeDtypeStruct((N_IDX_HEADS, t), F32),
    ]
    out_specs = [
        pl.BlockSpec((TM, wa), row), pl.BlockSpec((TM, wa), row),
        pl.BlockSpec((TM // TK, wa, TK), lambda i: (i, 0, 0)),
        pl.BlockSpec((TM, wi), row), pl.BlockSpec((TM, V7X_LANES), row), pl.BlockSpec((TM, V7X_LANES), row),
        pl.BlockSpec((N_IDX_HEADS, TM), lambda i: (0, i)),
    ]
    blk = (_nbytes((TM, d), BF16) + 3 * _nbytes((d, wa), BF16) + _nbytes((d, wi), BF16)
           + 2 * _nbytes((d, V7X_LANES), BF16) + 4 * _nbytes((TM, wa), BF16))
    return pl.pallas_call(
        _proj_kernel, out_shape=out_shape, grid=(t // TM,), in_specs=in_specs, out_specs=out_specs,
        compiler_params=_params(1, blk, 4 * _nbytes((TM, wa), F32)),
        name="projections",
    )(h, wq, wk, wvt, wqi, wklo, wkhi, wwt, gq, gk, gklo, gkhi)


def _sgu_kernel(h_ref, wu_ref, wv_ref, g_ref, wsp_ref, bspt_ref, o_ref):
    h = h_ref[...]
    u = jax.nn.gelu(jnp.dot(h, wu_ref[...], preferred_element_type=F32))
    v = jax.nn.gelu(jnp.dot(h, wv_ref[...], preferred_element_type=F32))
    ms = jnp.mean(v * v, axis=-1, keepdims=True)
    vn = (v * lax.rsqrt(ms + EPS) * g_ref[...]).astype(BF16)
    row = lax.broadcasted_iota(jnp.int32, (CHUNK, CHUNK), 0)
    col = lax.broadcasted_iota(jnp.int32, (CHUNK, CHUNK), 1)
    tril = col <= row
    bspt = bspt_ref[...]
    for g in range(N_GROUPS):
        wg = jnp.where(tril, wsp_ref[g], 0.0).astype(BF16)
        bg = jnp.broadcast_to(bspt[:, g:g + 1], (CHUNK, CHUNK))
        cs = slice(g * CHUNK, (g + 1) * CHUNK)
        for c in range(h.shape[0] // CHUNK):
            rs = slice(c * CHUNK, (c + 1) * CHUNK)
            z = jnp.dot(wg, vn[rs, cs], preferred_element_type=F32) + bg
            o_ref[rs, cs] = (u[rs, cs] * z).astype(o_ref.dtype)


def _sgu(h, wu, wv, g, wsp, bspt):
    t, d = h.shape
    wb = wu.shape[1]
    const2 = lambda i: (0, 0)
    blk = _nbytes((TM, d), BF16) + 2 * _nbytes((d, wb), BF16) + _nbytes((TM, wb), BF16) + _nbytes(wsp.shape, F32)
    return pl.pallas_call(
        _sgu_kernel,
        out_shape=jax.ShapeDtypeStruct((t, wb), BF16),
        grid=(t // TM,),
        in_specs=[pl.BlockSpec((TM, d), lambda i: (i, 0)),
                  pl.BlockSpec((d, wb), const2), pl.BlockSpec((d, wb), const2),
                  pl.BlockSpec((1, wb), const2),
                  pl.BlockSpec(wsp.shape, lambda i: (0, 0, 0)),
                  pl.BlockSpec(bspt.shape, const2)],
        out_specs=pl.BlockSpec((TM, wb), lambda i: (i, 0)),
        compiler_params=_params(1, blk, 5 * _nbytes((TM, wb), F32)),
        name="gmlp_branch",
    )(h, wu, wv, g, wsp, bspt)


def _bias_kernel(rb_ref, o_ref):
    ts = lax.broadcasted_iota(jnp.int32, (TK, TQ), 0)
    tq = lax.broadcasted_iota(jnp.int32, (TK, TQ), 1)
    max_exact = N_BUCKETS // 2
    for off in range(o_ref.shape[0]):
        dist = jnp.maximum(off * TQ + tq - ts, 0)
        d = jnp.maximum(dist.astype(F32), 1.0)
        large = max_exact + (jnp.log(d / max_exact) / math.log(MAX_DISTANCE / max_exact)
                             * (N_BUCKETS - max_exact)).astype(jnp.int32)
        large = jnp.minimum(large, N_BUCKETS - 1)
        bucket = jnp.where(dist < max_exact, dist, large)
        for hd in range(N_HEADS):
            acc = jnp.zeros((TK, TQ), F32)
            for b in range(N_BUCKETS):
                acc = jnp.where(bucket == b, rb_ref[b, hd], acc)
            o_ref[off, hd] = acc


def _bias_tables(rel_bias):
    n_off = 3
    assert 2 * TQ - TK + 1 >= MAX_DISTANCE
    return pl.pallas_call(
        _bias_kernel,
        out_shape=jax.ShapeDtypeStruct((n_off, N_HEADS, TK, TQ), F32),
        in_specs=[pl.BlockSpec(memory_space=pltpu.SMEM)],
        out_specs=pl.BlockSpec(memory_space=pltpu.VMEM),
        compiler_params=pltpu.CompilerParams(vmem_limit_bytes=32 << 20),
        name="t5_bias_tables",
    )(rel_bias)


def _bcast_rows(x8, rows):
    return jnp.broadcast_to(x8[None], (rows // V7X_SUBLANES,) + x8.shape).reshape(rows, x8.shape[1])


def _index_kernel(klo_ref, khi_ref, qi_ref, wt_ref, mask_ref, score_ref, *, n_sel):
    i = pl.program_id(1)
    n_kb = i + 1
    ts = lax.broadcasted_iota(jnp.int32, (TK, TQ), 0)
    tq = lax.broadcasted_iota(jnp.int32, (TK, TQ), 1)
    sub = (TK // V7X_SUBLANES, V7X_SUBLANES, TQ)

    def score_block(j, carry):
        mn, mx = carry
        r0 = pl.multiple_of(j * TK, TK)
        klo = klo_ref[pl.ds(r0, TK), :]
        khi = khi_ref[pl.ds(r0, TK), :]
        acc = jnp.zeros((TK, TQ), F32)
        for hp in range(N_IDX_HEADS // 2):
            qp = qi_ref[:, hp * 2 * IDX_DIM:(hp + 1) * 2 * IDX_DIM]
            s0 = lax.dot_general(klo, qp, _NT, preferred_element_type=F32)
            s1 = lax.dot_general(khi, qp, _NT, preferred_element_type=F32)
            acc = acc + wt_ref[2 * hp:2 * hp + 1, :] * jnp.maximum(s0, 0.0)
            acc = acc + wt_ref[2 * hp + 1:2 * hp + 2, :] * jnp.maximum(s1, 0.0)
        valid = (j * TK + ts) <= (i * TQ + tq)
        score_ref[pl.ds(r0, TK), :] = jnp.where(valid, acc, -jnp.inf)
        mx = jnp.maximum(mx, jnp.max(jnp.where(valid, acc, -jnp.inf).reshape(sub), axis=0))
        mn = jnp.minimum(mn, jnp.min(jnp.where(valid, acc, jnp.inf).reshape(sub), axis=0))
        return mn, mx

    init = (jnp.full((V7X_SUBLANES, TQ), jnp.inf, F32), jnp.full((V7X_SUBLANES, TQ), -jnp.inf, F32))
    mn8, mx8 = lax.fori_loop(0, n_kb, score_block, init)
    lo = jnp.broadcast_to(jnp.min(mn8, axis=0, keepdims=True), (V7X_SUBLANES, TQ))
    hi = jnp.broadcast_to(jnp.max(mx8, axis=0, keepdims=True), (V7X_SUBLANES, TQ))

    def bisect(_, carry):
        lo, hi = carry
        mid = lo + 0.5 * (hi - lo)

        def count_block(j, c8):
            r0 = pl.multiple_of(j * TK, TK)
            sc = score_ref[pl.ds(r0, TK), :].reshape(sub)
            return c8 + jnp.sum(jnp.where(sc >= mid[None], 1.0, 0.0), axis=0)

        c8 = lax.fori_loop(0, n_kb, count_block, jnp.zeros((V7X_SUBLANES, TQ), F32))
        cnt = jnp.broadcast_to(jnp.sum(c8, axis=0, keepdims=True), (V7X_SUBLANES, TQ))
        enough = cnt >= float(n_sel)
        return jnp.where(enough, mid, lo), jnp.where(enough, hi, mid)

    lo, hi = lax.fori_loop(0, BISECT_ITERS, bisect, (lo, hi))

    mask_ref[...] = jnp.zeros(mask_ref.shape, mask_ref.dtype)

    def write_block(j, carry):
        r0 = pl.multiple_of(j * TK, TK)
        sel = score_ref[pl.ds(r0, TK), :].reshape(sub) >= lo[None]
        mask_ref[pl.ds(r0, TK), :] = jnp.where(sel, 1.0, 0.0).reshape(TK, TQ).astype(mask_ref.dtype)
        return carry

    lax.fori_loop(0, n_kb, write_block, 0)


def _index_mask(klo, khi, qi, wt, batch, seq):
    n_q = seq // TQ
    n_sel = min(TOPK_MAX, seq // 4)
    wi = qi.shape[1]
    blk = (2 * _nbytes((seq, V7X_LANES), BF16) + _nbytes((TQ, wi), BF16) + _nbytes((N_IDX_HEADS, TQ), F32)
           + _nbytes((seq, TQ), BF16))
    return pl.pallas_call(
        functools.partial(_index_kernel, n_sel=n_sel),
        out_shape=jax.ShapeDtypeStruct((batch, seq, seq), BF16),
        grid=(batch, n_q),
        in_specs=[pl.BlockSpec((seq, V7X_LANES), lambda b, i: (b, 0)),
                  pl.BlockSpec((seq, V7X_LANES), lambda b, i: (b, 0)),
                  pl.BlockSpec((TQ, wi), lambda b, i: (b * n_q + i, 0)),
                  pl.BlockSpec((N_IDX_HEADS, TQ), lambda b, i: (0, b * n_q + i))],
        out_specs=pl.BlockSpec((None, seq, TQ), lambda b, i: (b, 0, i)),
        scratch_shapes=[pltpu.VMEM((seq, TQ), F32)],
        compiler_params=_params(2, blk, _nbytes((seq, TQ), F32) + 8 * _nbytes((TK, TQ), F32)),
        name="indexer_topk_mask",
    )(klo, khi, qi, wt)


def _attn_kernel(q_ref, k_ref, vt_ref, mask_ref, bias_ref, o_ref):
    i = pl.program_id(1)
    n_kb = i + 1
    n_off = bias_ref.shape[0]
    for hd in range(N_HEADS):
        hs = slice(hd * HEAD_DIM, (hd + 1) * HEAD_DIM)
        qh = q_ref[:, hs]

        def kv_block(j, carry):
            m, l, acc = carry
            r0 = pl.multiple_of(j * TK, TK)
            s = lax.dot_general(k_ref[pl.ds(r0, TK), hs], qh, _NT, preferred_element_type=F32)
            s = s + bias_ref[jnp.minimum(i - j, n_off - 1), hd]
            s = jnp.where(mask_ref[pl.ds(r0, TK), :] > 0, s, MASK_FILL)
            m_new = jnp.maximum(m, jnp.max(s, axis=0, keepdims=True))
            alpha = jnp.exp(m - m_new)
            p = jnp.exp(s - m_new)
            l = alpha * l + jnp.sum(p, axis=0, keepdims=True)
            pv = jnp.dot(vt_ref[j, hs, :], p.astype(BF16), preferred_element_type=F32)
            return m_new, l, alpha * acc + pv

        init = (jnp.full((1, TQ), MASK_FILL, F32), jnp.zeros((1, TQ), F32), jnp.zeros((HEAD_DIM, TQ), F32))
        m, l, acc = lax.fori_loop(0, n_kb, kv_block, init)
        o_ref[:, hs] = (acc / l).T.astype(o_ref.dtype)


def _attention(q, k, vt, mask, bias, batch, seq):
    n_q = seq // TQ
    wa = q.shape[1]
    blk = (_nbytes((TQ, wa), BF16) + 2 * _nbytes((seq, wa), BF16) + _nbytes((seq, TQ), BF16)
           + _nbytes(bias.shape, F32) + _nbytes((TQ, wa), BF16))
    return pl.pallas_call(
        _attn_kernel,
        out_shape=jax.ShapeDtypeStruct(q.shape, BF16),
        grid=(batch, n_q),
        in_specs=[pl.BlockSpec((TQ, wa), lambda b, i: (b * n_q + i, 0)),
                  pl.BlockSpec((seq, wa), lambda b, i: (b, 0)),
                  pl.BlockSpec((seq // TK, wa, TK), lambda b, i: (b, 0, 0)),
                  pl.BlockSpec((None, seq, TQ), lambda b, i: (b, 0, i)),
                  pl.BlockSpec(bias.shape, lambda b, i: (0, 0, 0, 0))],
        out_specs=pl.BlockSpec((TQ, wa), lambda b, i: (b * n_q + i, 0)),
        compiler_params=_params(2, blk, 10 * _nbytes((TK, TQ), F32)),
        name="masked_attention",
    )(q, k, vt, mask, bias)


def _merge_kernel(h_ref, oa_ref, ob_ref, wga_ref, wgb_ref, wpa_ref, wpb_ref, o_ref):
    h = h_ref[...]
    ga = jax.nn.sigmoid(jnp.dot(h, wga_ref[...], preferred_element_type=F32))
    a = jnp.dot(oa_ref[...], wpa_ref[...], preferred_element_type=F32)
    gb = jax.nn.sigmoid(jnp.dot(h, wgb_ref[...], preferred_element_type=F32))
    b = jnp.dot(ob_ref[...], wpb_ref[...], preferred_element_type=F32)
    o_ref[...] = (ga * a + gb * b).astype(o_ref.dtype)


def _merge(h, oa, ob, wga, wgb, wpa, wpb):
    t, d = h.shape
    wa, wb = oa.shape[1], ob.shape[1]
    tn = TN_MERGE
    col = lambda n, i: (0, n)
    row = lambda n, i: (i, 0)
    blk = (_nbytes((TM, d), BF16) + _nbytes((TM, wa + wb), BF16) + 2 * _nbytes((d, tn), BF16)
           + _nbytes((wa + wb, tn), BF16) + _nbytes((TM, tn), BF16))
    return pl.pallas_call(
        _merge_kernel,
        out_shape=jax.ShapeDtypeStruct((t, d), BF16),
        grid=(d // tn, t // TM),
        in_specs=[pl.BlockSpec((TM, d), row), pl.BlockSpec((TM, wa), row), pl.BlockSpec((TM, wb), row),
                  pl.BlockSpec((d, tn), col), pl.BlockSpec((d, tn), col),
                  pl.BlockSpec((wa, tn), col), pl.BlockSpec((wb, tn), col)],
        out_specs=pl.BlockSpec((TM, tn), lambda n, i: (i, n)),
        compiler_params=_params(2, blk, 5 * _nbytes((TM, tn), F32)),
        name="gated_merge",
    )(h, oa, ob, wga, wgb, wpa, wpb)


def _outproj_kernel(m_ref, x_ref, w_ref, g_ref, xo_ref, h2_ref):
    y = x_ref[...] + jnp.dot(m_ref[...], w_ref[...], preferred_element_type=F32)
    xo_ref[...] = y
    ms = jnp.mean(y * y, axis=-1, keepdims=True)
    h2_ref[...] = (y * lax.rsqrt(ms + EPS) * g_ref[...]).astype(h2_ref.dtype)


def _outproj(merged, x, w, g):
    t, d = x.shape
    row = lambda i: (i, 0)
    const = lambda i: (0, 0)
    blk = _nbytes((TM, d), BF16) * 2 + _nbytes((TM, d), F32) * 2 + _nbytes((d, d), BF16)
    return pl.pallas_call(
        _outproj_kernel,
        out_shape=[jax.ShapeDtypeStruct((t, d), F32), jax.ShapeDtypeStruct((t, d), BF16)],
        grid=(t // TM,),
        in_specs=[pl.BlockSpec((TM, d), row), pl.BlockSpec((TM, d), row),
                  pl.BlockSpec((d, d), const), pl.BlockSpec((1, d), const)],
        out_specs=[pl.BlockSpec((TM, d), row), pl.BlockSpec((TM, d), row)],
        compiler_params=_params(1, blk, 3 * _nbytes((TM, d), F32)),
        name="out_proj_norm2",
    )(merged, x, w, g)


def _ffn_kernel(h_ref, x_ref, wg_ref, wu_ref, wd_ref, o_ref, acc_ref):
    f = pl.program_id(1)

    @pl.when(f == 0)
    def _():
        acc_ref[...] = x_ref[...]

    h = h_ref[...]
    g = jnp.dot(h, wg_ref[...], preferred_element_type=F32)
    u = jnp.dot(h, wu_ref[...], preferred_element_type=F32)
    a = (jax.nn.silu(g) * u).astype(BF16)
    acc_ref[...] += jnp.dot(a, wd_ref[...], preferred_element_type=F32)

    @pl.when(f == pl.num_programs(1) - 1)
    def _():
        o_ref[...] = acc_ref[...]


def _ffn(h2, x_mid, wg, wu, wd):
    t, d = h2.shape
    ff = wg.shape[1]
    blk = (_nbytes((TM, d), BF16) + 2 * _nbytes((TM, d), F32) + 2 * _nbytes((d, TF), BF16) + _nbytes((TF, d), BF16))
    return pl.pallas_call(
        _ffn_kernel,
        out_shape=jax.ShapeDtypeStruct((t, d), F32),
        grid=(t // TM, ff // TF),
        in_specs=[pl.BlockSpec((TM, d), lambda i, f: (i, 0)), pl.BlockSpec((TM, d), lambda i, f: (i, 0)),
                  pl.BlockSpec((d, TF), lambda i, f: (0, f)), pl.BlockSpec((d, TF), lambda i, f: (0, f)),
                  pl.BlockSpec((TF, d), lambda i, f: (f, 0))],
        out_specs=pl.BlockSpec((TM, d), lambda i, f: (i, 0)),
        scratch_shapes=[pltpu.VMEM((TM, d), F32)],
        compiler_params=_params(2, blk, _nbytes((TM, d), F32) + 4 * _nbytes((TM, TF), F32)),
        name="swiglu_ffn",
    )(h2, x_mid, wg, wu, wd)


def _pad_cols(w, lo, width):
    return jnp.pad(w, ((0, 0), (lo, width - lo - w.shape[1])))


def kernel(x, norm1_g, w_in, q_norm_g, k_norm_g, idx_k_norm_g, sgu_norm_g, w_spatial, b_spatial,
           w_proj_a, w_proj_b, w_out, norm2_g, w_ffn_gate, w_ffn_up, w_ffn_down, rel_bias):
    batch, seq, d = x.shape
    depth = w_in.shape[0]
    wa = N_HEADS * HEAD_DIM
    wi = N_IDX_HEADS * IDX_DIM
    wb = N_GROUPS * CHUNK
    assert seq % TQ == 0 and seq % TM == 0 and TM % TK == 0 and TQ == TK and TM % CHUNK == 0
    assert w_ffn_gate.shape[2] % TF == 0 and d % TN_MERGE == 0
    splits = (wa, wa, wa, wi, IDX_DIM, N_IDX_HEADS, wb, wb, d, d)
    offs = [0]
    for s in splits:
        offs.append(offs[-1] + s)
    assert offs[-1] == w_in.shape[2]

    xf = x.reshape(batch * seq, d)
    bias_tab = _bias_tables(rel_bias)
    for l in range(depth):
        w = w_in[l]
        seg = [w[:, offs[n]:offs[n + 1]] for n in range(len(splits))]
        wq, wk, wv, wqi, wki, wwi, wub, wvb, wga, wgb = seg
        h = _rmsnorm(xf, norm1_g[l][None])
        gki = idx_k_norm_g[l][None]
        q, k, vt, qi, klo, khi, wt = _projections(
            h, wq.astype(BF16), wk.astype(BF16), wv.T.astype(BF16), wqi.astype(BF16),
            _pad_cols(wki, 0, V7X_LANES).astype(BF16), _pad_cols(wki, IDX_DIM, V7X_LANES).astype(BF16),
            wwi.T.astype(BF16),
            q_norm_g[l][None], k_norm_g[l][None],
            _pad_cols(gki, 0, V7X_LANES), _pad_cols(gki, IDX_DIM, V7X_LANES))
        out_b = _sgu(h, wub.astype(BF16), wvb.astype(BF16), sgu_norm_g[l][None], w_spatial[l], b_spatial[l].T)
        mask = _index_mask(klo, khi, qi, wt, batch, seq)
        out_a = _attention(q, k, vt, mask, bias_tab, batch, seq)
        merged = _merge(h, out_a, out_b, wga.astype(BF16), wgb.astype(BF16),
                        w_proj_a[l].astype(BF16), w_proj_b[l].astype(BF16))
        x_mid, h2 = _outproj(merged, xf, w_out[l].astype(BF16), norm2_g[l][None])
        xf = _ffn(h2, x_mid, w_ffn_gate[l].astype(BF16), w_ffn_up[l].astype(BF16), w_ffn_down[l].astype(BF16))
    return xf.reshape(batch, seq, d)
```
